```python
import math
import jax, jax.numpy as jnp
from jax import lax
import numpy as np

D_MODEL = 1024
BATCH = 8
SEQ = 2048
DEPTH = 2
DEC_BATCH = 128
DEC_SEQ = 8
PAST_LEN = 2048
PAGE_SIZE = 128

POOL_WIDTH = D_MODEL // 2
POOL_WINDOWS = (2, 4, 8, 16)
POOL_GROUP = POOL_WIDTH // len(POOL_WINDOWS)
POOL_BUF = max(POOL_WINDOWS) - 1
ATT_HEADS = 8
ATT_HEAD_DIM = 64
ATT_WIDTH = ATT_HEADS * ATT_HEAD_DIM
ATT_SCALE = ATT_HEAD_DIM ** -0.5
Q_BLOCK = 128
SSM_WIDTH = D_MODEL
SSM_HEAD_DIM = 64
SSM_HEADS = SSM_WIDTH // SSM_HEAD_DIM
SSM_GROUPS = 2
SSM_STATE = 128
CONV_WIDTH = 4
CONV_DIM = SSM_WIDTH + 2 * SSM_GROUPS * SSM_STATE
SSD_CHUNK = 128
N_BRANCH = 3
RMS_EPS = 1e-6
IN_SIZES = (POOL_WIDTH, POOL_WIDTH, ATT_WIDTH, ATT_WIDTH, ATT_WIDTH, ATT_HEADS, ATT_WIDTH,
            SSM_WIDTH, CONV_DIM, SSM_HEADS, N_BRANCH * D_MODEL)
D_IN = 2 * POOL_WIDTH + 4 * ATT_WIDTH + ATT_HEADS + SSM_WIDTH + CONV_DIM + SSM_HEADS + N_BRANCH * D_MODEL

kernel_name = 'pool_fox_ssd_gated_hybrid_step'


def rmsnorm(x, w):
    xf = x.astype(jnp.float32)
    y = xf * lax.rsqrt(jnp.mean(xf * xf, axis=-1, keepdims=True) + RMS_EPS)
    return (y * w.astype(jnp.float32)).astype(x.dtype)


def split_cols(h):
    parts, off = [], 0
    for n in IN_SIZES:
        parts.append(h[..., off:off + n])
        off += n
    return parts


def pool_mixer(u, buf, pos, w_pool, scale):
    bsz, L, _ = u.shape
    full = jnp.concatenate([buf.astype(u.dtype), u], axis=1)
    cs = jnp.cumsum(full.astype(jnp.float32), axis=1)
    cs0 = jnp.concatenate([jnp.zeros_like(cs[:, :1]), cs], axis=1)
    end = POOL_BUF + 1
    means = []
    for g, w in enumerate(POOL_WINDOWS):
        sl = slice(g * POOL_GROUP, (g + 1) * POOL_GROUP)
        wsum = cs0[:, end:end + L, sl] - cs0[:, end - w:end - w + L, sl]
        cnt = jnp.minimum(pos + 1, w).astype(jnp.float32)[None, :, None]
        means.append(wsum / cnt)
    mean = jnp.concatenate(means, axis=-1).astype(u.dtype)
    d = (mean - u).reshape(bsz, L, len(POOL_WINDOWS), POOL_GROUP)
    y = jnp.einsum('blgc,gcd->blgd', d, w_pool).reshape(bsz, L, POOL_WIDTH) * scale
    return y, full[:, -POOL_BUF:]


def fox_attend(q, k, v, cq, ck, qpos, kpos):
    s = jnp.einsum('bqhd,bkhd->bhqk', q, k).astype(jnp.float32) * ATT_SCALE
    s = s + jnp.swapaxes(cq, 1, 2)[..., None] - jnp.swapaxes(ck, 1, 2)[:, :, None, :]
    mask = kpos[None, :] <= qpos[:, None]
    p = jax.nn.softmax(jnp.where(mask, s, -jnp.inf), axis=-1)
    return jnp.einsum('bhqk,bkhd->bqhd', p.astype(v.dtype), v)


def fox_prompt(q, k, v, logf, pos):
    bsz, L, H, D = q.shape
    c = jnp.cumsum(logf, axis=1)
    nb = L // Q_BLOCK
    qb = jnp.swapaxes(q.reshape(bsz, nb, Q_BLOCK, H, D), 0, 1)
    cb = jnp.swapaxes(c.reshape(bsz, nb, Q_BLOCK, H), 0, 1)
    pb = pos.reshape(nb, Q_BLOCK)
    out = lax.map(lambda a: fox_attend(a[0], k, v, a[1], c, a[2], pos), (qb, cb, pb))
    return jnp.swapaxes(out, 0, 1).reshape(bsz, L, H, D)


def fox_sample(q, k, v, logf, k_past, v_past, logf_past, pos):
    L = q.shape[1]
    k_all = jnp.concatenate([k_past.astype(k.dtype), k], axis=1)
    v_all = jnp.concatenate([v_past.astype(v.dtype), v], axis=1)
    c_all = jnp.cumsum(jnp.concatenate([logf_past.astype(jnp.float32), logf], axis=1), axis=1)
    lk = k_all.shape[1]
    kpos = jnp.arange(lk, dtype=jnp.int32)
    return fox_attend(q, k_all, v_all, c_all[:, lk - L:], c_all, pos, kpos)


def causal_conv(xbc, buf, w, b):
    L = xbc.shape[1]
    full = jnp.concatenate([buf.astype(xbc.dtype), xbc], axis=1)
    y = b
    for j in range(CONV_WIDTH):
        y = y + full[:, j:j + L] * w[j]
    return jax.nn.silu(y), full[:, -(CONV_WIDTH - 1):]


def ssd_scan(x, dt, a, bm, cm, h0):
    bsz, L = x.shape[0], x.shape[1]
    chunk = SSD_CHUNK if L % SSD_CHUNK == 0 else L
    nc = L // chunk
    G, HG, P, N = SSM_GROUPS, SSM_HEADS // SSM_GROUPS, SSM_HEAD_DIM, SSM_STATE
    f32 = jnp.float32
    xc = x.astype(f32).reshape(bsz, nc, chunk, G, HG, P)
    dtc = dt.reshape(bsz, nc, chunk, G, HG)
    bc = bm.astype(f32).reshape(bsz, nc, chunk, G, N)
    cc = cm.astype(f32).reshape(bsz, nc, chunk, G, N)
    cum = jnp.cumsum(dtc * a.reshape(G, HG), axis=2)
    causal = jnp.tril(jnp.ones((chunk, chunk), dtype=bool))[:, :, None, None]
    seg = cum[:, :, :, None] - cum[:, :, None, :]
    decay = jnp.exp(jnp.where(causal, seg, -jnp.inf))
    cb = jnp.einsum('bctgn,bcsgn->bctsg', cc, bc)
    mix = cb[..., None] * decay * dtc[:, :, None]
    y = jnp.einsum('bctsgh,bcsghp->bctghp', mix, xc)
    xw = (jnp.exp(cum[:, :, -1:] - cum) * dtc)[..., None] * xc
    s_chunk = jnp.einsum('bcsgn,bcsghp->bcghpn', bc, xw)
    d_chunk = jnp.exp(cum[:, :, -1])

    def step(h, inp):
        s_c, d_c = inp
        return h * d_c[..., None, None] + s_c, h

    h_last, h_prev = lax.scan(step, h0.astype(f32).reshape(bsz, G, HG, P, N),
                              (jnp.moveaxis(s_chunk, 1, 0), jnp.moveaxis(d_chunk, 1, 0)))
    h_prev = jnp.moveaxis(h_prev, 0, 1)
    y = y + jnp.einsum('bctgn,bcghpn->bctghp', cc, h_prev) * jnp.exp(cum)[..., None]
    return (y.reshape(bsz, L, SSM_HEADS, P).astype(x.dtype),
            h_last.reshape(bsz, SSM_HEADS, P, N).astype(h0.dtype))


def gated_group_rmsnorm(y, z, w):
    bsz, L, C = y.shape
    g = (y * jax.nn.silu(z)).reshape(bsz, L, SSM_GROUPS, C // SSM_GROUPS)
    return rmsnorm(g, w.reshape(SSM_GROUPS, C // SSM_GROUPS)).reshape(bsz, L, C)


def mixer_layer(x, pos, norm_pre, w_in, pool_w, pool_scale, f_bias, conv_w, conv_b, dt_bias, a_log,
                d_skip, ssm_norm, w_branch_a, w_branch_b, w_branch_c, w_out, norm_post,
                pool_buf, conv_buf, ssm_h0, att_past):
    bsz, L, _ = x.shape
    hn = rmsnorm(x, norm_pre)
    u_a, z_a, q, k, v, f_lin, z_b, z_c, xbc, dt_raw, gates = split_cols(hn @ w_in)
    y_a, pool_new = pool_mixer(u_a, pool_buf, pos, pool_w, pool_scale)
    y_a = y_a * jax.nn.silu(z_a)
    q = q.reshape(bsz, L, ATT_HEADS, ATT_HEAD_DIM)
    k = k.reshape(bsz, L, ATT_HEADS, ATT_HEAD_DIM)
    v = v.reshape(bsz, L, ATT_HEADS, ATT_HEAD_DIM)
    logf = jax.nn.log_sigmoid(f_lin.astype(jnp.float32) + f_bias.astype(jnp.float32))
    if att_past is None:
        o_b = fox_prompt(q, k, v, logf, pos)
    else:
        o_b = fox_sample(q, k, v, logf, att_past[0], att_past[1], att_past[2], pos)
    y_b = o_b.reshape(bsz, L, ATT_WIDTH) * jax.nn.silu(z_b)
    xbc_c, conv_new = causal_conv(xbc, conv_buf, conv_w, conv_b)
    xs = xbc_c[..., :SSM_WIDTH].reshape(bsz, L, SSM_HEADS, SSM_HEAD_DIM)
    bm = xbc_c[..., SSM_WIDTH:SSM_WIDTH + SSM_GROUPS * SSM_STATE].reshape(bsz, L, SSM_GROUPS, SSM_STATE)
    cm = xbc_c[..., SSM_WIDTH + SSM_GROUPS * SSM_STATE:].reshape(bsz, L, SSM_GROUPS, SSM_STATE)
    dt = jax.nn.softplus(dt_raw.astype(jnp.float32) + dt_bias.astype(jnp.float32))
    a = -jnp.exp(a_log.astype(jnp.float32))
    y_c, h_new = ssd_scan(xs, dt, a, bm, cm, ssm_h0)
    y_c = (y_c + d_skip[:, None] * xs).reshape(bsz, L, SSM_WIDTH)
    y_c = gated_group_rmsnorm(y_c, z_c, ssm_norm)
    g = jax.nn.sigmoid(gates.reshape(bsz, L, N_BRANCH, D_MODEL))
    m = g[:, :, 0] * (y_a @ w_branch_a) + g[:, :, 1] * (y_b @ w_branch_b) + g[:, :, 2] * (y_c @ w_branch_c)
    out = x + rmsnorm(m @ w_out, norm_post)
    return out, (k, v, logf, pool_new, conv_new, h_new)


def setup_inputs(seed: int = 0) -> dict:
    key = jax.random.key(seed)
    ks = jax.random.split(key, 32)
    f32 = jnp.float32
    n_pages = PAST_LEN // PAGE_SIZE
    n_pool = (DEC_BATCH * n_pages * 5) // 4
    nrm = lambda k, s: jax.random.normal(k, s, f32)
    x_prompt = nrm(ks[0], (BATCH, SEQ, D_MODEL))
    x_sample = nrm(ks[1], (DEC_BATCH, DEC_SEQ, D_MODEL))
    cache_k = nrm(ks[2], (DEPTH, n_pool, PAGE_SIZE, ATT_HEADS, ATT_HEAD_DIM))
    cache_v = nrm(ks[3], (DEPTH, n_pool, PAGE_SIZE, ATT_HEADS, ATT_HEAD_DIM))
    cache_logf = jax.nn.log_sigmoid(2.0 + nrm(ks[4], (DEPTH, n_pool, PAGE_SIZE, ATT_HEADS)))
    state_pool = nrm(ks[5], (DEPTH, DEC_BATCH, POOL_BUF, POOL_WIDTH))
    state_conv = nrm(ks[6], (DEPTH, DEC_BATCH, CONV_WIDTH - 1, CONV_DIM))
    state_ssm = 0.5 * nrm(ks[7], (DEPTH, DEC_BATCH, SSM_HEADS, SSM_HEAD_DIM, SSM_STATE))
    perm = jax.random.permutation(ks[8], n_pool)
    page_table = perm[:DEC_BATCH * n_pages].reshape(DEC_BATCH, n_pages).astype(jnp.int32)
    norm_pre = 1.0 + 0.02 * nrm(ks[9], (DEPTH, D_MODEL))
    w_in = nrm(ks[10], (DEPTH, D_MODEL, D_IN)) * D_MODEL ** -0.5
    pool_w = nrm(ks[11], (DEPTH, len(POOL_WINDOWS), POOL_GROUP, POOL_GROUP)) * POOL_GROUP ** -0.5
    pool_scale = 1.0 + 0.02 * nrm(ks[12], (DEPTH, POOL_WIDTH))
    f_bias = jax.random.uniform(ks[13], (DEPTH, ATT_HEADS), f32, 1.0, 3.0)
    conv_w = nrm(ks[14], (DEPTH, CONV_WIDTH, CONV_DIM)) * CONV_WIDTH ** -0.5
    conv_b = 0.01 * nrm(ks[15], (DEPTH, CONV_DIM))
    dt0 = jnp.exp(jax.random.uniform(ks[16], (DEPTH, SSM_HEADS), f32, math.log(1e-3), math.log(1e-1)))
    dt_bias = dt0 + jnp.log(-jnp.expm1(-dt0))
    a_log = jnp.log(jax.random.uniform(ks[17], (DEPTH, SSM_HEADS), f32, 1.0, 16.0))
    d_skip = 1.0 + 0.01 * nrm(ks[18], (DEPTH, SSM_HEADS))
    ssm_norm = 1.0 + 0.02 * nrm(ks[19], (DEPTH, SSM_WIDTH))
    w_branch_a = nrm(ks[20], (DEPTH, POOL_WIDTH, D_MODEL)) * POOL_WIDTH ** -0.5
    w_branch_b = nrm(ks[21], (DEPTH, ATT_WIDTH, D_MODEL)) * ATT_WIDTH ** -0.5
    w_branch_c = nrm(ks[22], (DEPTH, SSM_WIDTH, D_MODEL)) * SSM_WIDTH ** -0.5
    w_out = nrm(ks[23], (DEPTH, D_MODEL, D_MODEL)) * D_MODEL ** -0.5
    norm_post = 1.0 + 0.02 * nrm(ks[24], (DEPTH, D_MODEL))
    return {'x_prompt': x_prompt, 'x_sample': x_sample, 'cache_k': cache_k, 'cache_v': cache_v,
            'cache_logf': cache_logf, 'state_pool': state_pool, 'state_conv': state_conv,
            'state_ssm': state_ssm, 'page_table': page_table, 'norm_pre': norm_pre, 'w_in': w_in,
            'pool_w': pool_w, 'pool_scale': pool_scale, 'f_bias': f_bias, 'conv_w': conv_w,
            'conv_b': conv_b, 'dt_bias': dt_bias, 'a_log': a_log, 'd_skip': d_skip,
            'ssm_norm': ssm_norm, 'w_branch_a': w_branch_a, 'w_branch_b': w_branch_b,
            'w_branch_c': w_branch_c, 'w_out': w_out, 'norm_post': norm_post}


def reference(x_prompt, x_sample, cache_k, cache_v, cache_logf, state_pool, state_conv, state_ssm,
              page_table, norm_pre, w_in, pool_w, pool_scale, f_bias, conv_w, conv_b, dt_bias, a_log,
              d_skip, ssm_norm, w_branch_a, w_branch_b, w_branch_c, w_out, norm_post):
    b_p, seq, _ = x_prompt.shape
    b_s, dec_seq, _ = x_sample.shape
    past_len = page_table.shape[1] * PAGE_SIZE
    pos_p = jnp.arange(seq, dtype=jnp.int32)
    pos_s = past_len + jnp.arange(dec_seq, dtype=jnp.int32)
    hp, hs = x_prompt, x_sample
    st_p, st_s = [], []
    for l in range(DEPTH):
        wts = (norm_pre[l], w_in[l], pool_w[l], pool_scale[l], f_bias[l], conv_w[l], conv_b[l],
               dt_bias[l], a_log[l], d_skip[l], ssm_norm[l], w_branch_a[l], w_branch_b[l],
               w_branch_c[l], w_out[l], norm_post[l])
        pool0 = jnp.zeros((b_p, POOL_BUF, POOL_WIDTH), hp.dtype)
        conv0 = jnp.zeros((b_p, CONV_WIDTH - 1, CONV_DIM), hp.dtype)
        ssm0 = jnp.zeros((b_p, SSM_HEADS, SSM_HEAD_DIM, SSM_STATE), state_ssm.dtype)
        hp, sp = mixer_layer(hp, pos_p, *wts, pool0, conv0, ssm0, None)
        k_past = cache_k[l][page_table].reshape(b_s, past_len, ATT_HEADS, ATT_HEAD_DIM)
        v_past = cache_v[l][page_table].reshape(b_s, past_len, ATT_HEADS, ATT_HEAD_DIM)
        lf_past = cache_logf[l][page_table].reshape(b_s, past_len, ATT_HEADS)
        hs, ss = mixer_layer(hs, pos_s, *wts, state_pool[l], state_conv[l], state_ssm[l],
                             (k_past, v_past, lf_past))
        st_p.append(sp)
        st_s.append(ss)
    k_p = jnp.stack([s[0] for s in st_p])
    v_p = jnp.stack([s[1] for s in st_p])
    lf_p = jnp.stack([s[2] for s in st_p])
    pool_p = jnp.stack([s[3] for s in st_p])
    conv_p = jnp.stack([s[4] for s in st_p])
    ssm_p = jnp.stack([s[5] for s in st_p])
    k_s = jnp.stack([s[0] for s in st_s])
    v_s = jnp.stack([s[1] for s in st_s])
    lf_s = jnp.stack([s[2] for s in st_s])
    pool_s = jnp.stack([s[3] for s in st_s])
    conv_s = jnp.stack([s[4] for s in st_s])
    ssm_s = jnp.stack([s[5] for s in st_s])
    return (hp, hs, k_p, v_p, lf_p, pool_p, conv_p, ssm_p, k_s, v_s, lf_s, pool_s, conv_s, ssm_s)
```

```python
import functools

import jax
import jax.numpy as jnp
from jax import lax
from jax.experimental import pallas as pl
from jax.experimental.pallas import tpu as pltpu

F32 = jnp.float32
BF16 = jnp.bfloat16
HIGHEST = lax.Precision.HIGHEST

D_MODEL = 1024
POOL_WIDTH = 512
POOL_WINDOWS = (2, 4, 8, 16)
POOL_GROUP = 128
POOL_BUF = 15
ATT_HEADS = 8
ATT_HEAD_DIM = 64
ATT_WIDTH = 512
ATT_SCALE = ATT_HEAD_DIM ** -0.5
PAGE_SIZE = 128
SSM_WIDTH = 1024
SSM_HEAD_DIM = 64
SSM_HEADS = 16
SSM_GROUPS = 2
SSM_STATE = 128
CONV_WIDTH = 4
CONV_DIM = 1536
SSD_CHUNK = 128
RMS_EPS = 1e-6
D_IN = 8728

LANES = 128
VMEM_LIMIT = 48 * 1024 * 1024

C_G = 0
C_ZC = 3072
C_UA = 4096
C_XBC = 4608
C_ZA = 6144
C_Q = 6656
C_K = 7168
C_V = 7680
C_ZB = 8192
C_MISC = 8704
NP = 9216
MISC_F = 0
MISC_DT = 8


def _sigmoid(x):
    return 1.0 / (1.0 + jnp.exp(-x))


def _silu(x):
    return x * _sigmoid(x)


def _log_sigmoid(x):
    return jnp.minimum(x, 0.0) - jnp.log1p(jnp.exp(-jnp.abs(x)))


def _softplus(x):
    return jnp.maximum(x, 0.0) + jnp.log1p(jnp.exp(-jnp.abs(x)))


def _cparams(sem):
    return pltpu.CompilerParams(dimension_semantics=sem, vmem_limit_bytes=VMEM_LIMIT)


def _inproj_kernel(x_ref, nw_ref, w_ref, o_ref, hn_ref):
    @pl.when(pl.program_id(1) == 0)
    def _():
        x = x_ref[...]
        ms = jnp.mean(x * x, axis=-1, keepdims=True)
        hn_ref[...] = (x * lax.rsqrt(ms + RMS_EPS) * nw_ref[...]).astype(BF16)

    o_ref[...] = jnp.dot(hn_ref[...], w_ref[...], preferred_element_type=F32)


def _inproj(x, norm_w, w_p):
    t = x.shape[0]
    tm = min(t, 1024)
    tn = 512
    return pl.pallas_call(
        _inproj_kernel,
        grid=(t // tm, NP // tn),
        in_specs=[
            pl.BlockSpec((tm, D_MODEL), lambda i, j: (i, 0)),
            pl.BlockSpec((1, D_MODEL), lambda i, j: (0, 0)),
            pl.BlockSpec((D_MODEL, tn), lambda i, j: (0, j)),
        ],
        out_specs=pl.BlockSpec((tm, tn), lambda i, j: (i, j)),
        out_shape=jax.ShapeDtypeStruct((t, NP), F32),
        scratch_shapes=[pltpu.VMEM((tm, D_MODEL), BF16)],
        compiler_params=_cparams(("arbitrary", "arbitrary")),
        name="inproj",
    )(x, norm_w, w_p)


def _logf_kernel(m_ref, fb_ref, lft_ref, crow_ref, ccol_ref, c_scr):
    seq = m_ref.shape[0]
    lf = _log_sigmoid(m_ref[...] + fb_ref[...])
    r = lax.broadcasted_iota(jnp.int32, (LANES, LANES), 0)
    c = lax.broadcasted_iota(jnp.int32, (LANES, LANES), 1)
    tri = (r >= c).astype(F32)
    carry = jnp.zeros((1, LANES), F32)
    for blk in range(seq // LANES):
        cb = jnp.dot(tri, lf[blk * LANES:(blk + 1) * LANES], precision=HIGHEST,
                     preferred_element_type=F32) + carry
        c_scr[blk * LANES:(blk + 1) * LANES, :] = cb
        carry = cb[LANES - 1:LANES, :]
    cum = c_scr[...]
    lft = lf.T
    cumt = cum.T
    lft_ref[...] = lft[:ATT_HEADS]
    for p in range(ATT_HEADS // 2):
        crow_ref[p] = cumt[2 * p:2 * p + 2]
    ccol_ref[...] = cum[:, :ATT_HEADS]


def _logf(proj, f_bias_pad, bsz, seq):
    t = bsz * seq
    return pl.pallas_call(
        _logf_kernel,
        grid=(bsz,),
        in_specs=[
            pl.BlockSpec((seq, LANES), lambda b: (b, C_MISC // LANES)),
            pl.BlockSpec((1, LANES), lambda b: (0, 0)),
        ],
        out_specs=[
            pl.BlockSpec((None, ATT_HEADS, seq), lambda b: (b, 0, 0)),
            pl.BlockSpec((None, ATT_HEADS // 2, 2, seq), lambda b: (b, 0, 0, 0)),
            pl.BlockSpec((seq, ATT_HEADS), lambda b: (b, 0)),
        ],
        out_shape=[
            jax.ShapeDtypeStruct((bsz, ATT_HEADS, seq), F32),
            jax.ShapeDtypeStruct((bsz, ATT_HEADS // 2, 2, seq), F32),
            jax.ShapeDtypeStruct((t, ATT_HEADS), F32),
        ],
        scratch_shapes=[pltpu.VMEM((seq, LANES), F32)],
        compiler_params=_cparams(("arbitrary",)),
        name="logf",
    )(proj, f_bias_pad)


def _pool_kernel(u_ref, za_ref, buf_ref, w_ref, sc_ref, y_ref, new_ref, full_ref, *, tp, pos0):
    i = pl.program_id(1)
    hist = POOL_BUF + 1

    @pl.when(i == 0)
    def _():
        full_ref[0:hist, :] = buf_ref[...]

    @pl.when(i > 0)
    def _():
        full_ref[0:hist, :] = full_ref[tp:tp + hist, :]

    u = u_ref[...]
    full_ref[hist:hist + tp, :] = u
    pos = pos0 + i * tp + lax.broadcasted_iota(jnp.int32, (tp, 1), 0)
    outs = []
    for g, w in enumerate(POOL_WINDOWS):
        cols = pl.ds(g * POOL_GROUP, POOL_GROUP)
        wsum = u[:, g * POOL_GROUP:(g + 1) * POOL_GROUP]
        for s in range(1, w):
            wsum = wsum + full_ref[pl.ds(hist - s, tp), cols]
        cnt = jnp.minimum(pos + 1, w).astype(F32)
        d = wsum / cnt - u[:, g * POOL_GROUP:(g + 1) * POOL_GROUP]
        outs.append(jnp.dot(d.astype(BF16), w_ref[g], preferred_element_type=F32))
    y = jnp.concatenate(outs, axis=-1) * sc_ref[...] * _silu(za_ref[...])
    y_ref[...] = y.astype(y_ref.dtype)

    @pl.when(i == pl.num_programs(1) - 1)
    def _():
        new_ref[...] = full_ref[tp:tp + hist, :]


def _pool(proj, buf16, pool_w, pool_scale, bsz, seq, pos0, out_dtype):
    tp = min(seq, 256)
    nt = seq // tp
    t = bsz * seq
    return pl.pallas_call(
        functools.partial(_pool_kernel, tp=tp, pos0=pos0),
        grid=(bsz, nt),
        in_specs=[
            pl.BlockSpec((tp, POOL_WIDTH), lambda b, i: (b * nt + i, C_UA // POOL_WIDTH)),
            pl.BlockSpec((tp, POOL_WIDTH), lambda b, i: (b * nt + i, C_ZA // POOL_WIDTH)),
            pl.BlockSpec((None, POOL_BUF + 1, POOL_WIDTH), lambda b, i: (b, 0, 0)),
            pl.BlockSpec((len(POOL_WINDOWS), POOL_GROUP, POOL_GROUP), lambda b, i: (0, 0, 0)),
            pl.BlockSpec((1, POOL_WIDTH), lambda b, i: (0, 0)),
        ],
        out_specs=[
            pl.BlockSpec((tp, POOL_WIDTH), lambda b, i: (b * nt + i, 0)),
            pl.BlockSpec((None, POOL_BUF + 1, POOL_WIDTH), lambda b, i: (b, 0, 0)),
        ],
        out_shape=[
            jax.ShapeDtypeStruct((t, POOL_WIDTH), out_dtype),
            jax.ShapeDtypeStruct((bsz, POOL_BUF + 1, POOL_WIDTH), F32),
        ],
        scratch_shapes=[pltpu.VMEM((POOL_BUF + 1 + tp, POOL_WIDTH), F32)],
        compiler_params=_cparams(("arbitrary", "arbitrary")),
        name="pool",
    )(proj, proj, buf16, pool_w, pool_scale)


def _attn_kernel(q_ref, k_ref, v_ref, zb_ref, ccol_ref, crow_ref, o_ref, acc_ref, *, tq):
    pair = pl.program_id(1)
    i = pl.program_id(2)
    lane = lax.broadcasted_iota(jnp.int32, (1, LANES), 1)
    lo = lane < ATT_HEAD_DIM
    q = q_ref[...] * ATT_SCALE
    qh = (jnp.where(lo, q, 0.0).astype(BF16), jnp.where(lo, 0.0, q).astype(BF16))
    ccol = ccol_ref[...]
    hl = lax.broadcasted_iota(jnp.int32, (1, ATT_HEADS), 1)
    cq = tuple(jnp.sum(jnp.where(hl == 2 * pair + a, ccol, 0.0), axis=-1, keepdims=True) for a in (0, 1))
    row = lax.broadcasted_iota(jnp.int32, (tq, tq), 0)
    col = lax.broadcasted_iota(jnp.int32, (tq, tq), 1)
    causal = col <= row
    acc_ref[...] = jnp.zeros_like(acc_ref)

    def tile(j, carry, masked):
        ks = pl.multiple_of(j * tq, tq)
        kt = k_ref[pl.ds(ks, tq), :].astype(BF16)
        vt = v_ref[pl.ds(ks, tq), :].astype(BF16)
        new = []
        for a in (0, 1):
            m_old, l_old = carry[2 * a], carry[2 * a + 1]
            s = lax.dot_general(qh[a], kt, (((1,), (1,)), ((), ())), preferred_element_type=F32)
            s = s + (cq[a] - crow_ref[a:a + 1, pl.ds(ks, tq)])
            if masked:
                s = jnp.where(causal, s, -jnp.inf)
            m_new = jnp.maximum(m_old, jnp.max(s, axis=-1, keepdims=True))
            alpha = jnp.exp(m_old - m_new)
            p = jnp.exp(s - m_new)
            l_new = alpha * l_old + jnp.sum(p, axis=-1, keepdims=True)
            acc_ref[a] = alpha * acc_ref[a] + jnp.dot(p.astype(BF16), vt, preferred_element_type=F32)
            new += [m_new, l_new]
        return tuple(new)

    init = (jnp.full((tq, 1), -jnp.inf, F32), jnp.zeros((tq, 1), F32)) * 2
    carry = lax.fori_loop(0, i, lambda j, c: tile(j, c, False), init)
    carry = tile(i, carry, True)
    o = jnp.where(lo, acc_ref[0] / carry[1], acc_ref[1] / carry[3])
    o_ref[...] = (o * _silu(zb_ref[...])).astype(o_ref.dtype)


def _attn_prompt(proj, ccol, crow, bsz, seq):
    tq = min(seq, 256)
    nq = seq // tq
    t = bsz * seq
    npair = ATT_HEADS // 2
    return pl.pallas_call(
        functools.partial(_attn_kernel, tq=tq),
        grid=(bsz, npair, nq),
        in_specs=[
            pl.BlockSpec((tq, LANES), lambda b, p, i: (b * nq + i, C_Q // LANES + p)),
            pl.BlockSpec((seq, LANES), lambda b, p, i: (b, C_K // LANES + p)),
            pl.BlockSpec((seq, LANES), lambda b, p, i: (b, C_V // LANES + p)),
            pl.BlockSpec((tq, LANES), lambda b, p, i: (b * nq + i, C_ZB // LANES + p)),
            pl.BlockSpec((tq, ATT_HEADS), lambda b, p, i: (b * nq + i, 0)),
            pl.BlockSpec((None, None, 2, seq), lambda b, p, i: (b, p, 0, 0)),
        ],
        out_specs=pl.BlockSpec((tq, LANES), lambda b, p, i: (b * nq + i, p)),
        out_shape=jax.ShapeDtypeStruct((t, ATT_WIDTH), BF16),
        scratch_shapes=[pltpu.VMEM((2, tq, LANES), F32)],
        compiler_params=_cparams(("arbitrary", "arbitrary", "arbitrary")),
        name="attn_prompt",
    )(proj, proj, proj, proj, ccol, crow)


def _ssd_kernel(xbc_ref, zc_ref, misc_ref, cbuf_ref, h0_ref, cw_ref, cb_ref, dtb_ref, alog_ref, dsk_ref,
                nw_ref, y_ref, cnew_ref, hlast_ref, conv_scr, xs_scr, dt_scr, da_scr, h_scr, y_scr, *, qr, qt):
    c = pl.program_id(1)
    halo = 8
    qs = SSD_CHUNK

    @pl.when(c == 0)
    def _():
        conv_scr[0:halo, :] = cbuf_ref[...]
        h_scr[...] = h0_ref[...]
        if qr < qs:
            xs_scr[...] = jnp.zeros_like(xs_scr)
            dt_scr[...] = jnp.zeros_like(dt_scr)
            da_scr[...] = jnp.zeros_like(da_scr)

    @pl.when(c > 0)
    def _():
        conv_scr[0:halo, :] = conv_scr[qr:qr + halo, :]

    conv_scr[halo:halo + qr, :] = xbc_ref[...]
    acc = cb_ref[...] + conv_scr[halo:halo + qr, :] * cw_ref[CONV_WIDTH - 1:CONV_WIDTH, :]
    for j in range(CONV_WIDTH - 1):
        acc = acc + conv_scr[pl.ds(halo - (CONV_WIDTH - 1) + j, qr), :] * cw_ref[j:j + 1, :]
    xs_scr[0:qr, :] = _silu(acc)

    dt = _softplus(misc_ref[...] + dtb_ref[...])
    a = -jnp.exp(alog_ref[...])
    dt_scr[0:qr, :] = dt
    da_scr[0:qr, :] = dt * a
    r = lax.broadcasted_iota(jnp.int32, (qs, qs), 0)
    cc = lax.broadcasted_iota(jnp.int32, (qs, qs), 1)
    tri = (r >= cc).astype(F32)
    cum = jnp.dot(tri, da_scr[...], precision=HIGHEST, preferred_element_type=F32)
    dtp = dt_scr[...]
    cum_last = cum[qr - 1:qr, :]
    wcol = jnp.exp(cum_last - cum) * dtp
    dch = jnp.exp(cum_last)
    cum_t = cum.T
    dt_t = dtp.T
    ecum = jnp.exp(cum[0:qt, :])
    causal = lax.broadcasted_iota(jnp.int32, (qt, qs), 1) <= lax.broadcasted_iota(jnp.int32, (qt, qs), 0)
    lane = lax.broadcasted_iota(jnp.int32, (1, LANES), 1)
    lo = lane < SSM_HEAD_DIM
    rlo = lax.broadcasted_iota(jnp.int32, (LANES, 1), 0) < SSM_HEAD_DIM

    for g in range(SSM_GROUPS):
        b_g = xs_scr[:, SSM_WIDTH + g * SSM_STATE:SSM_WIDTH + (g + 1) * SSM_STATE].astype(BF16)
        c_off = SSM_WIDTH + SSM_GROUPS * SSM_STATE + g * SSM_STATE
        c_g = xs_scr[0:qt, c_off:c_off + SSM_STATE].astype(BF16)
        cbm = lax.dot_general(c_g, b_g, (((1,), (1,)), ((), ())), preferred_element_type=F32)
        for jp in range(SSM_HEADS // SSM_GROUPS // 2):
            p = g * (SSM_HEADS // SSM_GROUPS // 2) + jp
            x_pair = xs_scr[:, p * LANES:(p + 1) * LANES]
            x_bf = x_pair.astype(BF16)
            ys = []
            for aa in (0, 1):
                hl = MISC_DT + 2 * p + aa
                seg = cum[0:qt, hl:hl + 1] - cum_t[hl:hl + 1, :]
                decay = jnp.exp(jnp.where(causal, seg, -jnp.inf))
                mix = (cbm * decay * dt_t[hl:hl + 1, :]).astype(BF16)
                ys.append(jnp.dot(mix, x_bf, preferred_element_type=F32))
            y = jnp.where(lo, ys[0], ys[1])
            h_prev = h_scr[p]
            y_int = lax.dot_general(c_g, h_prev.astype(BF16), (((1,), (1,)), ((), ())),
                                    preferred_element_type=F32)
            h0l = MISC_DT + 2 * p
            y = y + y_int * jnp.where(lo, ecum[:, h0l:h0l + 1], ecum[:, h0l + 1:h0l + 2])
            xw = x_pair * jnp.where(lo, wcol[:, h0l:h0l + 1], wcol[:, h0l + 1:h0l + 2])
            s_new = jnp.dot(xw.T.astype(BF16), b_g, preferred_element_type=F32)
            h_scr[p] = h_prev * jnp.where(rlo, dch[:, h0l:h0l + 1], dch[:, h0l + 1:h0l + 2]) + s_new
            y = y + dsk_ref[:, p * LANES:(p + 1) * LANES] * x_pair[0:qt]
            y_scr[:, p * LANES:(p + 1) * LANES] = y

    gated = y_scr[0:qr, :] * _silu(zc_ref[...])
    gw = SSM_WIDTH // SSM_GROUPS
    for g in range(SSM_GROUPS):
        gg = gated[:, g * gw:(g + 1) * gw]
        ms = jnp.mean(gg * gg, axis=-1, keepdims=True)
        y_ref[:, g * gw:(g + 1) * gw] = (gg * lax.rsqrt(ms + RMS_EPS) * nw_ref[:, g * gw:(g + 1) * gw]).astype(y_ref.dtype)

    @pl.when(c == pl.num_programs(1) - 1)
    def _():
        cnew_ref[...] = conv_scr[qr:qr + halo, :]
        hlast_ref[...] = h_scr[...]


def _ssd(proj, cbuf8, h0, conv_w, conv_b, dtb_pad, alog_pad, dskip_x, ssm_norm, bsz, seq, out_dtype):
    qr = SSD_CHUNK if seq % SSD_CHUNK == 0 else seq
    qt = qr
    nc = seq // qr
    t = bsz * seq
    npair = SSM_HEADS // 2
    const2 = lambda b, c: (0, 0)
    return pl.pallas_call(
        functools.partial(_ssd_kernel, qr=qr, qt=qt),
        grid=(bsz, nc),
        in_specs=[
            pl.BlockSpec((qr, CONV_DIM), lambda b, c: (b * nc + c, C_XBC // CONV_DIM)),
            pl.BlockSpec((qr, SSM_WIDTH), lambda b, c: (b * nc + c, C_ZC // SSM_WIDTH)),
            pl.BlockSpec((qr, LANES), lambda b, c: (b * nc + c, C_MISC // LANES)),
            pl.BlockSpec((None, 8, CONV_DIM), lambda b, c: (b, 0, 0)),
            pl.BlockSpec((None, npair, LANES, SSM_STATE), lambda b, c: (b, 0, 0, 0)),
            pl.BlockSpec((CONV_WIDTH, CONV_DIM), const2),
            pl.BlockSpec((1, CONV_DIM), const2),
            pl.BlockSpec((1, LANES), const2),
            pl.BlockSpec((1, LANES), const2),
            pl.BlockSpec((1, SSM_WIDTH), const2),
            pl.BlockSpec((1, SSM_WIDTH), const2),
        ],
        out_specs=[
            pl.BlockSpec((qr, SSM_WIDTH), lambda b, c: (b * nc + c, 0)),
            pl.BlockSpec((None, 8, CONV_DIM), lambda b, c: (b, 0, 0)),
            pl.BlockSpec((None, npair, LANES, SSM_STATE), lambda b, c: (b, 0, 0, 0)),
        ],
        out_shape=[
            jax.ShapeDtypeStruct((t, SSM_WIDTH), out_dtype),
            jax.ShapeDtypeStruct((bsz, 8, CONV_DIM), F32),
            jax.ShapeDtypeStruct((bsz, npair, LANES, SSM_STATE), F32),
        ],
        scratch_shapes=[
            pltpu.VMEM((8 + SSD_CHUNK, CONV_DIM), F32),
            pltpu.VMEM((SSD_CHUNK, CONV_DIM), F32),
            pltpu.VMEM((SSD_CHUNK, LANES), F32),
            pltpu.VMEM((SSD_CHUNK, LANES), F32),
            pltpu.VMEM((npair, LANES, SSM_STATE), F32),
            pltpu.VMEM((qt, SSM_WIDTH), F32),
        ],
        compiler_params=_cparams(("arbitrary", "arbitrary")),
        name="ssd",
    )(proj, proj, proj, cbuf8, h0, conv_w, conv_b, dtb_pad, alog_pad, dskip_x, ssm_norm)


def _attn_sample_kernel(pt_ref, q_ref, k_ref, v_ref, zb_ref, misc_ref, fb_ref, kc_hbm, vc_hbm, lc_hbm,
                        o_ref, lfn_ref, kbuf, vbuf, lbuf, s_scr, knew_scr, vnew_scr, tr_scr, sem,
                        *, layer, npages, dec):
    b = pl.program_id(0)
    nb = pl.num_programs(0)
    slot = b % 2
    nrow = dec * ATT_HEADS

    def copies(bb, sl):
        out = []
        for p in range(npages):
            pg = pt_ref[bb, p]
            out.append(pltpu.make_async_copy(kc_hbm.at[layer, pg], kbuf.at[sl, p], sem.at[0, sl]))
            out.append(pltpu.make_async_copy(vc_hbm.at[layer, pg], vbuf.at[sl, p], sem.at[1, sl]))
            out.append(pltpu.make_async_copy(lc_hbm.at[layer, pg], lbuf.at[sl, p], sem.at[2, sl]))
        return out

    @pl.when(b == 0)
    def _():
        for cp in copies(0, 0):
            cp.start()
        knew_scr[...] = jnp.zeros_like(knew_scr)
        vnew_scr[...] = jnp.zeros_like(vnew_scr)
        tr_scr[...] = jnp.zeros_like(tr_scr)

    @pl.when(b + 1 < nb)
    def _():
        for cp in copies(b + 1, 1 - slot):
            cp.start()

    lf_new = _log_sigmoid(misc_ref[...] + fb_ref[...])
    lfn_ref[...] = lf_new
    rows8 = lax.broadcasted_iota(jnp.int32, (dec, 1), 0)
    cnew = jnp.zeros_like(lf_new)
    for j in range(dec):
        cnew = cnew + jnp.where(rows8 >= j, lf_new[j:j + 1, :], 0.0)
    rown = lax.broadcasted_iota(jnp.int32, (nrow, 1), 0)
    lane = lax.broadcasted_iota(jnp.int32, (1, LANES), 1)
    cn_rep = jnp.concatenate([jnp.broadcast_to(cnew[t:t + 1, :], (ATT_HEADS, LANES)) for t in range(dec)], axis=0)
    cq_col = jnp.sum(jnp.where(lane == rown % ATT_HEADS, cn_rep, 0.0), axis=-1, keepdims=True)
    tr_scr[0:dec, :] = cnew
    cn_t = tr_scr[...].T
    cn_keys = jnp.concatenate([cn_t[0:ATT_HEADS, :]] * dec, axis=0)

    q = q_ref[...] * ATT_SCALE
    lane_w = lax.broadcasted_iota(jnp.int32, (1, ATT_WIDTH), 1)
    hmask8 = (lane_w // ATT_HEAD_DIM) == lax.broadcasted_iota(jnp.int32, (ATT_HEADS, 1), 0)
    qbd = jnp.concatenate(
        [jnp.where(hmask8, jnp.broadcast_to(q[t:t + 1, :], (ATT_HEADS, ATT_WIDTH)), 0.0) for t in range(dec)],
        axis=0).astype(BF16)

    knew_scr[0:dec, :] = k_ref[...]
    vnew_scr[0:dec, :] = v_ref[...]
    s_n = lax.dot_general(qbd, knew_scr[...].astype(BF16), (((1,), (1,)), ((), ())), preferred_element_type=F32)
    s_n = s_n + (cq_col - cn_keys)
    s_n = jnp.where(lane <= rown // ATT_HEADS, s_n, -jnp.inf)
    s_scr[:, npages * PAGE_SIZE:(npages + 1) * PAGE_SIZE] = s_n

    for cp in copies(b, slot):
        cp.wait()

    l2 = lbuf[slot].reshape(npages * ATT_HEADS, PAGE_SIZE)
    nr = npages * ATT_HEADS
    rr = lax.broadcasted_iota(jnp.int32, (PAGE_SIZE, PAGE_SIZE), 0)
    cc = lax.broadcasted_iota(jnp.int32, (PAGE_SIZE, PAGE_SIZE), 1)
    su = (rr > cc).astype(F32)
    r_in = jnp.dot(l2, su, precision=HIGHEST, preferred_element_type=F32)
    tot = jnp.dot(l2, jnp.ones((PAGE_SIZE, PAGE_SIZE), F32), precision=HIGHEST, preferred_element_type=F32)
    r2 = lax.broadcasted_iota(jnp.int32, (nr, nr), 0)
    c2 = lax.broadcasted_iota(jnp.int32, (nr, nr), 1)
    m2 = ((c2 > r2) & ((c2 - r2) % ATT_HEADS == 0)).astype(F32)
    rsuf = r_in + jnp.dot(m2, tot, precision=HIGHEST, preferred_element_type=F32)

    for p in range(npages):
        kt = kbuf[slot, p].astype(BF16)
        s_p = jnp.dot(qbd, kt, preferred_element_type=F32)
        bias = jnp.concatenate([rsuf[p * ATT_HEADS:(p + 1) * ATT_HEADS, :]] * dec, axis=0) + cq_col
        s_scr[:, p * PAGE_SIZE:(p + 1) * PAGE_SIZE] = s_p + bias

    s_all = s_scr[...]
    m = jnp.max(s_all, axis=-1, keepdims=True)
    pr = jnp.exp(s_all - m)
    den = jnp.sum(pr, axis=-1, keepdims=True)
    s_scr[...] = pr
    acc = jnp.dot(s_scr[:, npages * PAGE_SIZE:(npages + 1) * PAGE_SIZE].astype(BF16), vnew_scr[...].astype(BF16),
                  preferred_element_type=F32)
    for p in range(npages):
        vt = vbuf[slot, p].astype(BF16)
        pp = s_scr[:, p * PAGE_SIZE:(p + 1) * PAGE_SIZE].astype(BF16)
        acc = acc + lax.dot_general(pp, vt, (((1,), (1,)), ((), ())), preferred_element_type=F32)
    bmask = (lane_w // ATT_HEAD_DIM) == (rown % ATT_HEADS)
    od = jnp.where(bmask, acc / den, 0.0)
    o = jnp.sum(od.reshape(dec, ATT_HEADS, ATT_WIDTH), axis=1)
    o_ref[...] = (o * _silu(zb_ref[...])).astype(o_ref.dtype)


def _attn_sample(proj, page_table, f_bias_pad, kc, vc, lc, layer, bsz, dec):
    npages = page_table.shape[1]
    t = bsz * dec
    nrow = dec * ATT_HEADS
    any_spec = pl.BlockSpec(memory_space=pl.ANY)
    grid_spec = pltpu.PrefetchScalarGridSpec(
        num_scalar_prefetch=1,
        grid=(bsz,),
        in_specs=[
            pl.BlockSpec((dec, ATT_WIDTH), lambda b, pt: (b, C_Q // ATT_WIDTH)),
            pl.BlockSpec((dec, ATT_WIDTH), lambda b, pt: (b, C_K // ATT_WIDTH)),
            pl.BlockSpec((dec, ATT_WIDTH), lambda b, pt: (b, C_V // ATT_WIDTH)),
            pl.BlockSpec((dec, ATT_WIDTH), lambda b, pt: (b, C_ZB // ATT_WIDTH)),
            pl.BlockSpec((dec, LANES), lambda b, pt: (b, C_MISC // LANES)),
            pl.BlockSpec((1, LANES), lambda b, pt: (0, 0)),
            any_spec, any_spec, any_spec,
        ],
        out_specs=[
            pl.BlockSpec((dec, ATT_WIDTH), lambda b, pt: (b, 0)),
            pl.BlockSpec((dec, LANES), lambda b, pt: (b, 0)),
        ],
        scratch_shapes=[
            pltpu.VMEM((2, npages, ATT_WIDTH, PAGE_SIZE), F32),
            pltpu.VMEM((2, npages, ATT_WIDTH, PAGE_SIZE), F32),
            pltpu.VMEM((2, npages, ATT_HEADS, PAGE_SIZE), F32),
            pltpu.VMEM((nrow, (npages + 1) * PAGE_SIZE), F32),
            pltpu.VMEM((PAGE_SIZE, ATT_WIDTH), F32),
            pltpu.VMEM((PAGE_SIZE, ATT_WIDTH), F32),
            pltpu.VMEM((LANES, LANES), F32),
            pltpu.SemaphoreType.DMA((3, 2)),
        ],
    )
    return pl.pallas_call(
        functools.partial(_attn_sample_kernel, layer=layer, npages=npages, dec=dec),
        grid_spec=grid_spec,
        out_shape=[
            jax.ShapeDtypeStruct((t, ATT_WIDTH), F32),
            jax.ShapeDtypeStruct((t, LANES), F32),
        ],
        compiler_params=_cparams(("arbitrary",)),
        name="attn_sample",
    )(page_table, proj, proj, proj, proj, proj, f_bias_pad, kc, vc, lc)


def _merge_kernel(x_ref, g0_ref, g1_ref, g2_ref, ya_ref, yb_ref, yc_ref, wa_ref, wb_ref, wc_ref, wo_ref,
                  nw_ref, o_ref):
    m = _sigmoid(g0_ref[...]) * jnp.dot(ya_ref[...].astype(BF16), wa_ref[...], preferred_element_type=F32)
    m = m + _sigmoid(g1_ref[...]) * jnp.dot(yb_ref[...].astype(BF16), wb_ref[...], preferred_element_type=F32)
    m = m + _sigmoid(g2_ref[...]) * jnp.dot(yc_ref[...].astype(BF16), wc_ref[...], preferred_element_type=F32)
    o = jnp.dot(m.astype(BF16), wo_ref[...], preferred_element_type=F32)
    ms = jnp.mean(o * o, axis=-1, keepdims=True)
    o_ref[...] = x_ref[...] + o * lax.rsqrt(ms + RMS_EPS) * nw_ref[...]


def _merge(x, proj, ya, yb, yc, wa, wb, wc, wo, norm_post):
    t = x.shape[0]
    tm = min(t, 256)
    const = lambda i: (0, 0)
    return pl.pallas_call(
        _merge_kernel,
        grid=(t // tm,),
        in_specs=[
            pl.BlockSpec((tm, D_MODEL), lambda i: (i, 0)),
            pl.BlockSpec((tm, D_MODEL), lambda i: (i, 0)),
            pl.BlockSpec((tm, D_MODEL), lambda i: (i, 1)),
            pl.BlockSpec((tm, D_MODEL), lambda i: (i, 2)),
            pl.BlockSpec((tm, POOL_WIDTH), lambda i: (i, 0)),
            pl.BlockSpec((tm, ATT_WIDTH), lambda i: (i, 0)),
            pl.BlockSpec((tm, SSM_WIDTH), lambda i: (i, 0)),
            pl.BlockSpec((POOL_WIDTH, D_MODEL), const),
            pl.BlockSpec((ATT_WIDTH, D_MODEL), const),
            pl.BlockSpec((SSM_WIDTH, D_MODEL), const),
            pl.BlockSpec((D_MODEL, D_MODEL), const),
            pl.BlockSpec((1, D_MODEL), const),
        ],
        out_specs=pl.BlockSpec((tm, D_MODEL), lambda i: (i, 0)),
        out_shape=jax.ShapeDtypeStruct((t, D_MODEL), F32),
        compiler_params=_cparams(("arbitrary",)),
        name="merge",
    )(x, proj, proj, proj, ya, yb, yc, wa, wb, wc, wo, norm_post)


def _repack_w_in(w):
    o = 0
    seg = {}
    for name, n in (("ua", 512), ("za", 512), ("q", 512), ("k", 512), ("v", 512), ("f", 8), ("zb", 512),
                    ("zc", 1024), ("xbc", 1536), ("dt", 16), ("g", 3072)):
        seg[name] = w[:, o:o + n]
        o += n
    pad = jnp.zeros((w.shape[0], NP - C_MISC - ATT_HEADS - SSM_HEADS), w.dtype)
    cols = [seg["g"], seg["zc"], seg["ua"], seg["xbc"], seg["za"], seg["q"], seg["k"], seg["v"], seg["zb"],
            seg["f"], seg["dt"], pad]
    return jnp.concatenate(cols, axis=1).astype(BF16)


def _pad_lanes(v, off):
    out = jnp.zeros((1, LANES), F32)
    return out.at[0, off:off + v.shape[0]].set(v.astype(F32))


def kernel(x_prompt, x_sample, cache_k, cache_v, cache_logf, state_pool, state_conv, state_ssm, page_table,
           norm_pre, w_in, pool_w, pool_scale, f_bias, conv_w, conv_b, dt_bias, a_log, d_skip, ssm_norm,
           w_branch_a, w_branch_b, w_branch_c, w_out, norm_post):
    b_p, seq, _ = x_prompt.shape
    b_s, dec, _ = x_sample.shape
    depth = w_in.shape[0]
    n_pool = cache_k.shape[1]
    past_len = page_table.shape[1] * PAGE_SIZE
    npair = SSM_HEADS // 2

    kc = jnp.transpose(cache_k, (0, 1, 3, 4, 2)).reshape(depth, n_pool, ATT_WIDTH, PAGE_SIZE)
    vc = jnp.transpose(cache_v, (0, 1, 3, 4, 2)).reshape(depth, n_pool, ATT_WIDTH, PAGE_SIZE)
    lc = jnp.transpose(cache_logf, (0, 1, 3, 2))

    hp = x_prompt.reshape(b_p * seq, D_MODEL)
    hs = x_sample.reshape(b_s * dec, D_MODEL)
    zeros_pool = jnp.zeros((b_p, POOL_BUF + 1, POOL_WIDTH), F32)
    zeros_conv = jnp.zeros((b_p, 8, CONV_DIM), F32)
    zeros_ssm = jnp.zeros((b_p, npair, LANES, SSM_STATE), F32)
    outs = {k: [] for k in ("k_p", "v_p", "lf_p", "pool_p", "conv_p", "ssm_p",
                            "k_s", "v_s", "lf_s", "pool_s", "conv_s", "ssm_s")}
    for l in range(depth):
        w_p = _repack_w_in(w_in[l])
        nw = norm_pre[l].reshape(1, D_MODEL)
        fb = _pad_lanes(f_bias[l], MISC_F)
        dtb = _pad_lanes(dt_bias[l], MISC_DT)
        alog = _pad_lanes(a_log[l], MISC_DT)
        dsk = jnp.repeat(d_skip[l].astype(F32), SSM_HEAD_DIM).reshape(1, SSM_WIDTH)
        pw = pool_w[l].astype(BF16)
        psc = pool_scale[l].reshape(1, POOL_WIDTH)
        cw = conv_w[l]
        cb = conv_b[l].reshape(1, CONV_DIM)
        snw = ssm_norm[l].reshape(1, SSM_WIDTH)
        wa = w_branch_a[l].astype(BF16)
        wb = w_branch_b[l].astype(BF16)
        wc = w_branch_c[l].astype(BF16)
        wo = w_out[l].astype(BF16)
        npost = norm_post[l].reshape(1, D_MODEL)

        proj = _inproj(hp, nw, w_p)
        lft, crow, ccol = _logf(proj, fb, b_p, seq)
        ya, pool_new = _pool(proj, zeros_pool, pw, psc, b_p, seq, 0, BF16)
        yb = _attn_prompt(proj, ccol, crow, b_p, seq)
        yc, conv_new, h_new = _ssd(proj, zeros_conv, zeros_ssm, cw, cb, dtb, alog, dsk, snw, b_p, seq, BF16)
        hp = _merge(hp, proj, ya, yb, yc, wa, wb, wc, wo, npost)
        outs["k_p"].append(proj[:, C_K:C_K + ATT_WIDTH].reshape(b_p, seq, ATT_HEADS, ATT_HEAD_DIM))
        outs["v_p"].append(proj[:, C_V:C_V + ATT_WIDTH].reshape(b_p, seq, ATT_HEADS, ATT_HEAD_DIM))
        outs["lf_p"].append(jnp.transpose(lft, (0, 2, 1)))
        outs["pool_p"].append(pool_new[:, 1:])
        outs["conv_p"].append(conv_new[:, 8 - (CONV_WIDTH - 1):])
        outs["ssm_p"].append(h_new.reshape(b_p, SSM_HEADS, SSM_HEAD_DIM, SSM_STATE))

        proj_s = _inproj(hs, nw, w_p)
        pbuf = jnp.pad(state_pool[l], ((0, 0), (1, 0), (0, 0)))
        cbuf = jnp.pad(state_conv[l], ((0, 0), (8 - (CONV_WIDTH - 1), 0), (0, 0)))
        h0 = state_ssm[l].reshape(b_s, npair, LANES, SSM_STATE)
        ya_s, pool_new_s = _pool(proj_s, pbuf, pw, psc, b_s, dec, past_len, F32)
        yb_s, lfn = _attn_sample(proj_s, page_table, fb, kc, vc, lc, l, b_s, dec)
        yc_s, conv_new_s, h_new_s = _ssd(proj_s, cbuf, h0, cw, cb, dtb, alog, dsk, snw, b_s, dec, F32)
        hs = _merge(hs, proj_s, ya_s, yb_s, yc_s, wa, wb, wc, wo, npost)
        outs["k_s"].append(proj_s[:, C_K:C_K + ATT_WIDTH].reshape(b_s, dec, ATT_HEADS, ATT_HEAD_DIM))
        outs["v_s"].append(proj_s[:, C_V:C_V + ATT_WIDTH].reshape(b_s, dec, ATT_HEADS, ATT_HEAD_DIM))
        outs["lf_s"].append(lfn[:, :ATT_HEADS].reshape(b_s, dec, ATT_HEADS))
        outs["pool_s"].append(pool_new_s[:, 1:])
        outs["conv_s"].append(conv_new_s[:, 8 - (CONV_WIDTH - 1):])
        outs["ssm_s"].append(h_new_s.reshape(b_s, SSM_HEADS, SSM_HEAD_DIM, SSM_STATE))

    st = {k: jnp.stack(v) for k, v in outs.items()}
    return (hp.reshape(b_p, seq, D_MODEL), hs.reshape(b_s, dec, D_MODEL),
            st["k_p"], st["v_p"], st["lf_p"], st["pool_p"], st["conv_p"], st["ssm_p"],
            st["k_s"], st["v_s"], st["lf_s"], st["pool_s"], st["conv_s"], st["ssm_s"])
```
